```python
import jax, jax.numpy as jnp
from jax import lax
import numpy as np

D_MODEL = 2048
BATCH = 2
SEQ = 16384
DEPTH = 2
DEC_BATCH = 4
DEC_SEQ = 4096
PAST_LEN = 128

GRID_W = 64
POOL_WINDOWS = (2, 4, 8, 16)
N_POOL_GROUPS = len(POOL_WINDOWS)
POOL_GROUP_DIM = D_MODEL // 8
POOL_DIM = N_POOL_GROUPS * POOL_GROUP_DIM
N_HEADS = 16
HEAD_DIM = D_MODEL // 32
ATTN_DIM = N_HEADS * HEAD_DIM
WIN_ROWS = 8
WIN_COLS = 16
N_BRANCH = 2
IN_DIM = POOL_DIM + 3 * ATTN_DIM + N_BRANCH * D_MODEL
D_FF = 256 * ((8 * D_MODEL // 3 + 255) // 256)
CONV_WIDTH = 3
EPS = 1e-6

kernel_name = "hybrid_pool_natten_convglu_encoder"


def rmsnorm(x, g):
    xf = x.astype(jnp.float32)
    y = xf * lax.rsqrt(jnp.mean(xf * xf, axis=-1, keepdims=True) + EPS)
    return (y * g.astype(jnp.float32)).astype(x.dtype)


def pool_mixer(z, pool_w, pool_scale):
    n = z.shape[1]
    zf = z.astype(jnp.float32)
    cs = jnp.concatenate([jnp.zeros_like(zf[:, :1]), jnp.cumsum(zf, axis=1)], axis=1)
    t = jnp.arange(n)
    outs = []
    for g, w in enumerate(POOL_WINDOWS):
        sl = slice(g * POOL_GROUP_DIM, (g + 1) * POOL_GROUP_DIM)
        lo = jnp.clip(t - w // 2, 0, n - 1)
        hi = jnp.clip(t + (w - w // 2) - 1, 0, n - 1)
        csg = cs[:, :, sl]
        win_sum = jnp.take(csg, hi + 1, axis=1) - jnp.take(csg, lo, axis=1)
        cnt = (hi - lo + 1).astype(jnp.float32)[None, :, None]
        pooled = (win_sum / cnt - zf[:, :, sl]).astype(z.dtype)
        outs.append(jnp.einsum('bnc,cd->bnd', pooled, pool_w[g]))
    return jnp.concatenate(outs, axis=-1) * pool_scale


def neighbourhood_attention(q, k, v, rpb):
    B, n, H, Dh = q.shape
    rows = n // GRID_W
    kr = min(WIN_ROWS, rows)
    scale = HEAD_DIM ** -0.5
    qg = q.reshape(B, rows, GRID_W, H, Dh)
    kg = k.reshape(B, rows, GRID_W, H, Dh)
    vg = v.reshape(B, rows, GRID_W, H, Dh)
    row_ids = jnp.arange(rows)
    row_start = jnp.clip(row_ids - kr // 2, 0, rows - kr)
    cols = jnp.arange(GRID_W)
    col_start = jnp.clip(cols - WIN_COLS // 2, 0, GRID_W - WIN_COLS)
    col_idx = col_start[:, None] + jnp.arange(WIN_COLS)[None, :]
    dc = col_idx - cols[:, None]

    def row_step(args):
        q_row, s, r = args
        k_blk = lax.dynamic_slice_in_dim(kg, s, kr, axis=1)
        v_blk = lax.dynamic_slice_in_dim(vg, s, kr, axis=1)
        k_sel = k_blk[:, :, col_idx]
        v_sel = v_blk[:, :, col_idx]
        dr = s + jnp.arange(kr) - r
        bias = rpb[:, dr + WIN_ROWS - 1][:, :, dc + WIN_COLS - 1]
        bias = jnp.transpose(bias, (0, 2, 1, 3)).astype(jnp.float32)
        scores = jnp.einsum('bchd,bicjhd->bhcij', q_row, k_sel,
                            preferred_element_type=jnp.float32) * scale + bias[None]
        p = jax.nn.softmax(scores.reshape(B, H, GRID_W, kr * WIN_COLS), axis=-1)
        p = p.reshape(B, H, GRID_W, kr, WIN_COLS).astype(v.dtype)
        return jnp.einsum('bhcij,bicjhd->bchd', p, v_sel)

    q_rows = jnp.transpose(qg, (1, 0, 2, 3, 4))
    out = lax.map(row_step, (q_rows, row_start, row_ids))
    return jnp.transpose(out, (1, 0, 2, 3, 4)).reshape(B, n, H * Dh)


def mixer_block(u, w_in, pool_w, pool_scale, rpb, w_pool_br, w_attn_br, w_out):
    B, n, _ = u.shape
    proj = u @ w_in
    splits = [POOL_DIM, POOL_DIM + ATTN_DIM, POOL_DIM + 2 * ATTN_DIM,
              POOL_DIM + 3 * ATTN_DIM, POOL_DIM + 3 * ATTN_DIM + D_MODEL]
    z_pool, q, k, v, g_pool, g_attn = jnp.split(proj, splits, axis=-1)
    pool_out = pool_mixer(z_pool, pool_w, pool_scale) @ w_pool_br
    attn = neighbourhood_attention(q.reshape(B, n, N_HEADS, HEAD_DIM),
                                   k.reshape(B, n, N_HEADS, HEAD_DIM),
                                   v.reshape(B, n, N_HEADS, HEAD_DIM), rpb)
    attn_out = attn @ w_attn_br
    merged = jax.nn.sigmoid(g_pool) * pool_out + jax.nn.sigmoid(g_attn) * attn_out
    return merged @ w_out


def conv_glu_ffn(h, w_up, conv_w, conv_b, w_down):
    a = h @ w_up
    ap = jnp.pad(a, ((0, 0), (1, 1), (0, 0)))
    a = ap[:, :-2] * conv_w[0] + ap[:, 1:-1] * conv_w[1] + ap[:, 2:] * conv_w[2] + conv_b
    gate, val = jnp.split(a, 2, axis=-1)
    return (jax.nn.gelu(gate, approximate=False) * val) @ w_down


def trunk(x, norm1_g, w_in, pool_w, pool_scale, rpb, w_pool_br, w_attn_br, w_out,
          norm2_g, w_up, conv_w, conv_b, w_down, norm_f):
    for l in range(DEPTH):
        x = x + mixer_block(rmsnorm(x, norm1_g[l]), w_in[l], pool_w[l], pool_scale[l],
                            rpb[l], w_pool_br[l], w_attn_br[l], w_out[l])
        x = x + conv_glu_ffn(rmsnorm(x, norm2_g[l]), w_up[l], conv_w[l], conv_b[l], w_down[l])
    return rmsnorm(x, norm_f)


def setup_inputs(seed: int = 0) -> dict:
    key = jax.random.key(seed)
    ks = jax.random.split(key, 16)
    f32 = jnp.float32
    L = DEPTH
    nrm = lambda k, shape, s: jax.random.normal(k, shape, f32) * s
    return {
        "x_prompt": nrm(ks[0], (BATCH, SEQ, D_MODEL), 1.0),
        "x_sample": nrm(ks[1], (DEC_BATCH, DEC_SEQ, D_MODEL), 1.0),
        "norm1_g": 1.0 + nrm(ks[2], (L, D_MODEL), 0.05),
        "w_in": nrm(ks[3], (L, D_MODEL, IN_DIM), D_MODEL ** -0.5),
        "pool_w": nrm(ks[4], (L, N_POOL_GROUPS, POOL_GROUP_DIM, POOL_GROUP_DIM), POOL_GROUP_DIM ** -0.5),
        "pool_scale": 1.0 + nrm(ks[5], (L, POOL_DIM), 0.1),
        "rpb": nrm(ks[6], (L, N_HEADS, 2 * WIN_ROWS - 1, 2 * WIN_COLS - 1), 0.1),
        "w_pool_br": nrm(ks[7], (L, POOL_DIM, D_MODEL), POOL_DIM ** -0.5),
        "w_attn_br": nrm(ks[8], (L, ATTN_DIM, D_MODEL), ATTN_DIM ** -0.5),
        "w_out": nrm(ks[9], (L, D_MODEL, D_MODEL), D_MODEL ** -0.5),
        "norm2_g": 1.0 + nrm(ks[10], (L, D_MODEL), 0.05),
        "w_up": nrm(ks[11], (L, D_MODEL, 2 * D_FF), D_MODEL ** -0.5),
        "conv_w": nrm(ks[12], (L, CONV_WIDTH, 2 * D_FF), CONV_WIDTH ** -0.5),
        "conv_b": nrm(ks[13], (L, 2 * D_FF), 0.02),
        "w_down": nrm(ks[14], (L, D_FF, D_MODEL), D_FF ** -0.5),
        "norm_f": 1.0 + nrm(ks[15], (D_MODEL,), 0.05),
    }


def reference(x_prompt, x_sample, norm1_g, w_in, pool_w, pool_scale, rpb, w_pool_br, w_attn_br,
              w_out, norm2_g, w_up, conv_w, conv_b, w_down, norm_f):
    y_prompt = trunk(x_prompt, norm1_g, w_in, pool_w, pool_scale, rpb, w_pool_br, w_attn_br,
                     w_out, norm2_g, w_up, conv_w, conv_b, w_down, norm_f)
    y_sample = trunk(x_sample, norm1_g, w_in, pool_w, pool_scale, rpb, w_pool_br, w_attn_br,
                     w_out, norm2_g, w_up, conv_w, conv_b, w_down, norm_f)
    return (y_prompt, y_sample)
```

```python
import functools
import math

import jax
import jax.numpy as jnp
from jax import lax
from jax.experimental import pallas as pl
from jax.experimental.pallas import tpu as pltpu

GRID_W = 64
POOL_WINDOWS = (2, 4, 8, 16)
N_HEADS = 16
HEAD_DIM = 64
WIN_ROWS = 8
WIN_COLS = 16
EPS = 1e-6
NEG_BIG = -1e30

V7X_LANES = 128
V7X_BF16_SUBLANES = 16
V7X_VMEM_LIMIT_BYTES = 58 * 1024 * 1024

F32 = jnp.float32
BF16 = jnp.bfloat16


def _rmsnorm_f32(x, g):
    return x * lax.rsqrt(jnp.mean(x * x, axis=-1, keepdims=True) + EPS) * g


def _params(semantics):
    return pltpu.CompilerParams(dimension_semantics=semantics,
                                vmem_limit_bytes=V7X_VMEM_LIMIT_BYTES)


def _inproj_kernel(x_ref, g_ref, w_ref, o_ref, u_scr):
    @pl.when(pl.program_id(1) == 0)
    def _():
        u_scr[...] = _rmsnorm_f32(x_ref[...], g_ref[...]).astype(BF16)

    o_ref[...] = jnp.dot(u_scr[...], w_ref[...],
                         preferred_element_type=F32).astype(o_ref.dtype)


def _inproj(x, g, w, *, tm, tn):
    t, d = x.shape
    n = w.shape[1]
    return pl.pallas_call(
        _inproj_kernel,
        grid=(t // tm, n // tn),
        in_specs=[
            pl.BlockSpec((tm, d), lambda i, j: (i, 0)),
            pl.BlockSpec((1, d), lambda i, j: (0, 0)),
            pl.BlockSpec((d, tn), lambda i, j: (0, j)),
        ],
        out_specs=pl.BlockSpec((tm, tn), lambda i, j: (i, j)),
        out_shape=jax.ShapeDtypeStruct((t, n), BF16),
        scratch_shapes=[pltpu.VMEM((tm, d), BF16)],
        compiler_params=_params(("parallel", "arbitrary")),
        name="inproj",
    )(x, g, w)


def _attn_bias_table(rpb):
    c = jnp.arange(GRID_W)
    col_start = jnp.clip(c - WIN_COLS // 2, 0, GRID_W - WIN_COLS)
    lane = jnp.arange(2 * GRID_W)
    cc = lane % GRID_W
    i_local = lane // GRID_W
    dd = jnp.arange(2 * WIN_ROWS - 2)
    valid = (cc[None, :] >= col_start[:, None]) & (cc[None, :] < col_start[:, None] + WIN_COLS)
    dc = jnp.clip(cc[None, :] - c[:, None] + WIN_COLS - 1, 0, 2 * WIN_COLS - 2)
    dr = dd[:, None] + i_local[None, :]
    tab = rpb[:, dr[:, None, :], dc[None, :, :]]
    return jnp.where(valid[None, None], tab.astype(F32), NEG_BIG)


def _attn_kernel(q_ref, kp_ref, kc_ref, kn_ref, vp_ref, vc_ref, vn_ref, bias_ref, o_ref,
                 k_scr, v_scr, *, rows, rblk):
    tq = rblk * GRID_W
    k_scr[0:tq] = kp_ref[...]
    k_scr[tq:2 * tq] = kc_ref[...]
    k_scr[2 * tq:3 * tq] = kn_ref[...]
    v_scr[0:tq] = vp_ref[...]
    v_scr[tq:2 * tq] = vc_ref[...]
    v_scr[2 * tq:3 * tq] = vn_ref[...]

    row0 = pl.program_id(1) * rblk
    lane = lax.broadcasted_iota(jnp.int32, (GRID_W, V7X_LANES), 1)
    first = lane < HEAD_DIM
    scale = HEAD_DIM ** -0.5

    def row_body(j, carry):
        r = row0 + j
        rs = jnp.clip(r - WIN_ROWS // 2, 0, rows - WIN_ROWS)
        off = r - rs
        kstart = pl.multiple_of((rs - row0 + rblk) * GRID_W, GRID_W)
        qrow = pl.multiple_of(j * GRID_W, GRID_W)
        for p in range(N_HEADS // 2):
            cols = slice(p * V7X_LANES, (p + 1) * V7X_LANES)
            q2 = q_ref[pl.ds(qrow, GRID_W), cols].astype(F32) * scale
            kw = k_scr[pl.ds(kstart, WIN_ROWS * GRID_W), cols]
            vw = v_scr[pl.ds(kstart, WIN_ROWS * GRID_W), cols]
            outs = []
            for e in range(2):
                h = 2 * p + e
                qh = jnp.where(first if e == 0 else jnp.logical_not(first), q2, 0.0).astype(BF16)
                s = lax.dot_general(qh, kw, (((1,), (1,)), ((), ())),
                                    preferred_element_type=F32)
                b = jnp.concatenate(
                    [bias_ref[h, WIN_ROWS - 1 - off + 2 * jj] for jj in range(WIN_ROWS // 2)],
                    axis=-1)
                s = s + b
                m = jnp.max(s, axis=-1, keepdims=True)
                pexp = jnp.exp(s - m)
                l = jnp.sum(pexp, axis=-1, keepdims=True)
                o = jnp.dot(pexp.astype(BF16), vw, preferred_element_type=F32)
                outs.append(o / l)
            o_ref[pl.ds(qrow, GRID_W), cols] = jnp.where(first, outs[0], outs[1]).astype(o_ref.dtype)
        return carry

    lax.fori_loop(0, rblk, row_body, 0)


def _attention(proj, bias, *, seqlen, rblk):
    t = proj.shape[0]
    rows = seqlen // GRID_W
    nblk = rows // rblk
    tq = rblk * GRID_W
    ad = N_HEADS * HEAD_DIM

    def cur(col):
        return lambda s, b: (s * nblk + b, col)

    def prev(col):
        return lambda s, b: (s * nblk + jnp.maximum(b - 1, 0), col)

    def nxt(col):
        return lambda s, b: (s * nblk + jnp.minimum(b + 1, nblk - 1), col)

    blk = (tq, ad)
    return pl.pallas_call(
        functools.partial(_attn_kernel, rows=rows, rblk=rblk),
        grid=(t // seqlen, nblk),
        in_specs=[
            pl.BlockSpec(blk, cur(1)),
            pl.BlockSpec(blk, prev(2)), pl.BlockSpec(blk, cur(2)), pl.BlockSpec(blk, nxt(2)),
            pl.BlockSpec(blk, prev(3)), pl.BlockSpec(blk, cur(3)), pl.BlockSpec(blk, nxt(3)),
            pl.BlockSpec(bias.shape, lambda s, b: (0, 0, 0, 0)),
        ],
        out_specs=pl.BlockSpec(blk, lambda s, b: (s * nblk + b, 0)),
        out_shape=jax.ShapeDtypeStruct((t, ad), BF16),
        scratch_shapes=[pltpu.VMEM((3 * tq, ad), BF16), pltpu.VMEM((3 * tq, ad), BF16)],
        compiler_params=_params(("parallel", "arbitrary")),
        name="attention",
    )(proj, proj, proj, proj, proj, proj, proj, bias)


def _merge_kernel(z_ref, zp_ref, zn_ref, gp_ref, ga_ref, attn_ref, x_ref,
                  pw_ref, ps_ref, wpb_ref, wab_ref, wo_ref, o_ref, zext, pf_scr, *, seqlen, tm):
    halo = V7X_BF16_SUBLANES
    pos0 = (pl.program_id(0) * tm) % seqlen
    zc = z_ref[...].astype(F32)
    zext[0:halo] = jnp.where(pos0 == 0, 0.0, zp_ref[...].astype(F32))
    zext[halo:halo + tm] = zc
    zext[halo + tm:2 * halo + tm] = jnp.where(pos0 + tm == seqlen, 0.0, zn_ref[...].astype(F32))

    pos = pos0 + lax.broadcasted_iota(jnp.int32, (tm, 1), 0)
    gdim = zc.shape[1] // len(POOL_WINDOWS)
    for g, w in enumerate(POOL_WINDOWS):
        cols = slice(g * gdim, (g + 1) * gdim)
        lo = jnp.maximum(pos - w // 2, 0)
        hi = jnp.minimum(pos + (w - w // 2) - 1, seqlen - 1)
        cnt = (hi - lo + 1).astype(F32)
        win = zext[halo - w // 2:halo - w // 2 + tm, cols]
        for d in range(-(w // 2) + 1, w - w // 2):
            win = win + zext[halo + d:halo + d + tm, cols]
        pooled = (win / cnt - zc[:, cols]).astype(BF16)
        pf = jnp.dot(pooled, pw_ref[g], preferred_element_type=F32) * ps_ref[:, cols]
        pf_scr[:, cols] = pf.astype(BF16)

    pool_out = jnp.dot(pf_scr[...], wpb_ref[...], preferred_element_type=F32)
    attn_out = jnp.dot(attn_ref[...], wab_ref[...], preferred_element_type=F32)
    merged = (jax.nn.sigmoid(gp_ref[...].astype(F32)) * pool_out
              + jax.nn.sigmoid(ga_ref[...].astype(F32)) * attn_out)
    o_ref[...] = x_ref[...] + jnp.dot(merged.astype(BF16), wo_ref[...],
                                      preferred_element_type=F32)


def _merge(proj, attn, x, pool_w, pool_scale, w_pool_br, w_attn_br, w_out, *, seqlen, tm):
    t, d = x.shape
    pd = pool_scale.shape[1]
    ad = attn.shape[1]
    halo = V7X_BF16_SUBLANES
    hb = tm // halo
    nh = t // halo
    const2 = lambda i: (0, 0)
    return pl.pallas_call(
        functools.partial(_merge_kernel, seqlen=seqlen, tm=tm),
        grid=(t // tm,),
        in_specs=[
            pl.BlockSpec((tm, pd), lambda i: (i, 0)),
            pl.BlockSpec((halo, pd), lambda i: (jnp.maximum(i * hb - 1, 0), 0)),
            pl.BlockSpec((halo, pd), lambda i: (jnp.minimum((i + 1) * hb, nh - 1), 0)),
            pl.BlockSpec((tm, d), lambda i: (i, 2)),
            pl.BlockSpec((tm, d), lambda i: (i, 3)),
            pl.BlockSpec((tm, ad), lambda i: (i, 0)),
            pl.BlockSpec((tm, d), lambda i: (i, 0)),
            pl.BlockSpec(pool_w.shape, lambda i: (0, 0, 0)),
            pl.BlockSpec(pool_scale.shape, const2),
            pl.BlockSpec(w_pool_br.shape, const2),
            pl.BlockSpec(w_attn_br.shape, const2),
            pl.BlockSpec(w_out.shape, const2),
        ],
        out_specs=pl.BlockSpec((tm, d), lambda i: (i, 0)),
        out_shape=jax.ShapeDtypeStruct((t, d), F32),
        scratch_shapes=[pltpu.VMEM((tm + 2 * halo, pd), F32), pltpu.VMEM((tm, pd), BF16)],
        compiler_params=_params(("parallel",)),
        name="merge",
    )(proj, proj, proj, proj, proj, attn, x, pool_w, pool_scale, w_pool_br, w_attn_br, w_out)


def _ffn_kernel(x_ref, xp_ref, xn_ref, g_ref, wg_ref, wv_ref, cwg_ref, cwv_ref, cbg_ref, cbv_ref,
                wd_ref, gf_ref, o_ref, h_scr, *, seqlen, tm, final):
    halo = V7X_BF16_SUBLANES
    j = pl.program_id(1)

    @pl.when(j == 0)
    def _():
        pos0 = (pl.program_id(0) * tm) % seqlen
        g = g_ref[...]
        hp = _rmsnorm_f32(xp_ref[...], g)
        hn = _rmsnorm_f32(xn_ref[...], g)
        h_scr[0:halo] = jnp.where(pos0 == 0, 0.0, hp).astype(BF16)
        h_scr[halo:halo + tm] = _rmsnorm_f32(x_ref[...], g).astype(BF16)
        h_scr[halo + tm:2 * halo + tm] = jnp.where(pos0 + tm == seqlen, 0.0, hn).astype(BF16)
        o_ref[...] = x_ref[...]

    h = h_scr[...]

    def conv(w_ref, cw_ref, cb_ref):
        a = jnp.dot(h, w_ref[...], preferred_element_type=F32)
        return (a[halo - 1:halo - 1 + tm] * cw_ref[0:1, :] + a[halo:halo + tm] * cw_ref[1:2, :]
                + a[halo + 1:halo + 1 + tm] * cw_ref[2:3, :] + cb_ref[...])

    gate = conv(wg_ref, cwg_ref, cbg_ref)
    val = conv(wv_ref, cwv_ref, cbv_ref)
    act = gate * (lax.erf(gate / math.sqrt(2.0)) + 1.0) / 2.0
    o_ref[...] += jnp.dot((act * val).astype(BF16), wd_ref[...], preferred_element_type=F32)

    if final:
        @pl.when(j == pl.num_programs(1) - 1)
        def _():
            o_ref[...] = _rmsnorm_f32(o_ref[...], gf_ref[...])


def _ffn(x, g, w_up, conv_w, conv_b, w_down, g_final, *, seqlen, tm, tf, final):
    t, d = x.shape
    f = w_down.shape[0]
    nf = f // tf
    halo = V7X_BF16_SUBLANES
    hb = tm // halo
    nh = t // halo
    return pl.pallas_call(
        functools.partial(_ffn_kernel, seqlen=seqlen, tm=tm, final=final),
        grid=(t // tm, nf),
        in_specs=[
            pl.BlockSpec((tm, d), lambda i, j: (i, 0)),
            pl.BlockSpec((halo, d), lambda i, j: (jnp.maximum(i * hb - 1, 0), 0)),
            pl.BlockSpec((halo, d), lambda i, j: (jnp.minimum((i + 1) * hb, nh - 1), 0)),
            pl.BlockSpec((1, d), lambda i, j: (0, 0)),
            pl.BlockSpec((d, tf), lambda i, j: (0, j)),
            pl.BlockSpec((d, tf), lambda i, j: (0, j + nf)),
            pl.BlockSpec((3, tf), lambda i, j: (0, j)),
            pl.BlockSpec((3, tf), lambda i, j: (0, j + nf)),
            pl.BlockSpec((1, tf), lambda i, j: (0, j)),
            pl.BlockSpec((1, tf), lambda i, j: (0, j + nf)),
            pl.BlockSpec((tf, d), lambda i, j: (j, 0)),
            pl.BlockSpec((1, d), lambda i, j: (0, 0)),
        ],
        out_specs=pl.BlockSpec((tm, d), lambda i, j: (i, 0)),
        out_shape=jax.ShapeDtypeStruct((t, d), F32),
        scratch_shapes=[pltpu.VMEM((tm + 2 * halo, d), BF16)],
        compiler_params=_params(("parallel", "arbitrary")),
        name="ffn",
    )(x, x, x, g, w_up, w_up, conv_w, conv_w, conv_b, conv_b, w_down, g_final)


def _tile(n, target):
    tile = min(n, target)
    assert n % tile == 0, (n, tile)
    return tile


def _trunk(x3, layers, norm_f):
    b, seqlen, d = x3.shape
    assert seqlen % (WIN_ROWS * GRID_W) == 0, seqlen
    x = x3.reshape(b * seqlen, d)
    depth = len(layers)
    for l, p in enumerate(layers):
        proj = _inproj(x, p["norm1_g"], p["w_in"], tm=_tile(seqlen, 1024), tn=1024)
        attn = _attention(proj, p["bias"], seqlen=seqlen, rblk=WIN_ROWS)
        x = _merge(proj, attn, x, p["pool_w"], p["pool_scale"], p["w_pool_br"], p["w_attn_br"],
                   p["w_out"], seqlen=seqlen, tm=_tile(seqlen, 256))
        x = _ffn(x, p["norm2_g"], p["w_up"], p["conv_w"], p["conv_b"], p["w_down"], norm_f,
                 seqlen=seqlen, tm=_tile(seqlen, 512), tf=512, final=(l == depth - 1))
    return x.reshape(b, seqlen, d)


def kernel(x_prompt, x_sample, norm1_g, w_in, pool_w, pool_scale, rpb, w_pool_br, w_attn_br, w_out,
           norm2_g, w_up, conv_w, conv_b, w_down, norm_f):
    depth = w_in.shape[0]
    layers = []
    for l in range(depth):
        layers.append(dict(
            norm1_g=norm1_g[l][None, :],
            w_in=w_in[l].astype(BF16),
            pool_w=pool_w[l].astype(BF16),
            pool_scale=pool_scale[l][None, :],
            bias=_attn_bias_table(rpb[l]),
            w_pool_br=w_pool_br[l].astype(BF16),
            w_attn_br=w_attn_br[l].astype(BF16),
            w_out=w_out[l].astype(BF16),
            norm2_g=norm2_g[l][None, :],
            w_up=w_up[l].astype(BF16),
            conv_w=conv_w[l],
            conv_b=conv_b[l][None, :],
            w_down=w_down[l].astype(BF16),
        ))
    nf = norm_f[None, :]
    return (_trunk(x_prompt, layers, nf), _trunk(x_sample, layers, nf))
```
